```python
import math
import jax, jax.numpy as jnp
from jax import lax
import numpy as np

D_MODEL = 1024
BATCH = 16
SEQ = 2048
DEPTH = 2
DEC_BATCH = 32
DEC_SEQ = 2048
PAST_LEN = 128

DIFF_HEADS = 4
DIFF_DH = 64
DIFF_DV = 2 * DIFF_DH
DIFF_WIDTH = DIFF_HEADS * DIFF_DV
MLA_HEADS = 4
MLA_NOPE = 128
MLA_ROPE = 64
MLA_V = 128
Q_LORA = 256
KV_LORA = 128
MLA_WIDTH = MLA_HEADS * MLA_V
MIX_WIDTH = DIFF_WIDTH + MLA_WIDTH
IN_COLS = 3 * DIFF_WIDTH + Q_LORA + KV_LORA + MLA_ROPE
D_FF = ((8 * D_MODEL + 3 * 256 - 1) // (3 * 256)) * 256
ROPE_THETA = 10000.0
Q_BLOCK = 128
EPS = 1e-6

kernel_name = "hybrid_diffattn_mla_encoder"


def _rms(x, g):
    xf = x.astype(jnp.float32)
    y = xf * lax.rsqrt(jnp.mean(xf * xf, axis=-1, keepdims=True) + EPS)
    return (y * g.astype(jnp.float32)).astype(x.dtype)


def _rope(x, pos):
    d = x.shape[-1]
    half = d // 2
    inv = 1.0 / (ROPE_THETA ** (jnp.arange(half, dtype=jnp.float32) * 2.0 / d))
    ang = pos.astype(jnp.float32)[:, None] * inv[None, :]
    cos = jnp.cos(ang)[None, :, None, :]
    sin = jnp.sin(ang)[None, :, None, :]
    xf = x.astype(jnp.float32)
    x1, x2 = xf[..., :half], xf[..., half:]
    out = jnp.concatenate([x1 * cos - x2 * sin, x2 * cos + x1 * sin], axis=-1)
    return out.astype(x.dtype)


def _to_blocks(t):
    b, s = t.shape[:2]
    return t.reshape((b, s // Q_BLOCK, Q_BLOCK) + t.shape[2:]).swapaxes(0, 1)


def _from_blocks(t):
    nb, b, qb = t.shape[:3]
    return t.swapaxes(0, 1).reshape((b, nb * qb) + t.shape[3:])


def _diff_attention(q, k, v, lam, subln_g, lam_init):
    b, s = q.shape[:2]
    q = q.reshape(b, s, DIFF_HEADS, 2, DIFF_DH)
    k = k.reshape(b, s, DIFF_HEADS, 2, DIFF_DH)
    scale = DIFF_DH ** -0.5

    def blk(qb):
        sc = jnp.einsum('bqhcd,bkhcd->bhcqk', qb, k).astype(jnp.float32) * scale
        p = jax.nn.softmax(sc, axis=-1)
        a = p[:, :, 0] - lam * p[:, :, 1]
        return jnp.einsum('bhqk,bkhe->bqhe', a.astype(v.dtype), v)

    o = _from_blocks(lax.map(blk, _to_blocks(q)))
    o = _rms(o, subln_g) * (1.0 - lam_init)
    return o.reshape(b, s, DIFF_WIDTH)


def _mla_attention(q_nope, q_pe, k_nope, k_pe, v):
    b, s = q_nope.shape[:2]
    scale = (MLA_NOPE + MLA_ROPE) ** -0.5

    def blk(qs):
        qn, qp = qs
        sc = (jnp.einsum('bqhd,bkhd->bhqk', qn, k_nope)
              + jnp.einsum('bqhr,bkr->bhqk', qp, k_pe)).astype(jnp.float32) * scale
        p = jax.nn.softmax(sc, axis=-1)
        return jnp.einsum('bhqk,bkhd->bqhd', p.astype(v.dtype), v)

    o = _from_blocks(lax.map(blk, (_to_blocks(q_nope), _to_blocks(q_pe))))
    return o.reshape(b, s, MLA_WIDTH)


def _layer(x, pos, l, attn_norm, w_in, lam_q1, lam_k1, lam_q2, lam_k2, diff_subln,
           q_a_norm, w_q_b, kv_a_norm, w_kv_b, w_o, ffn_norm, w_gate, w_up, w_down):
    b, s, _ = x.shape
    h = _rms(x, attn_norm[l])
    z = h @ w_in[l]
    o1 = DIFF_WIDTH
    o2 = o1 + DIFF_WIDTH
    o3 = o2 + DIFF_WIDTH
    o4 = o3 + Q_LORA
    o5 = o4 + KV_LORA
    dq = _rope(z[..., :o1].reshape(b, s, 2 * DIFF_HEADS, DIFF_DH), pos)
    dk = _rope(z[..., o1:o2].reshape(b, s, 2 * DIFF_HEADS, DIFF_DH), pos)
    dv = z[..., o2:o3].reshape(b, s, DIFF_HEADS, DIFF_DV)
    c_q = z[..., o3:o4]
    c_kv = z[..., o4:o5]
    k_pe = z[..., o5:]

    lam_init = 0.8 - 0.6 * math.exp(-0.3 * l)
    lam = (jnp.exp(jnp.sum(lam_q1[l].astype(jnp.float32) * lam_k1[l].astype(jnp.float32)))
           - jnp.exp(jnp.sum(lam_q2[l].astype(jnp.float32) * lam_k2[l].astype(jnp.float32)))
           + lam_init)
    y_diff = _diff_attention(dq, dk, dv, lam, diff_subln[l], lam_init)

    qh = (_rms(c_q, q_a_norm[l]) @ w_q_b[l]).reshape(b, s, MLA_HEADS, MLA_NOPE + MLA_ROPE)
    q_nope = qh[..., :MLA_NOPE]
    q_pe = _rope(qh[..., MLA_NOPE:], pos)
    kvh = (_rms(c_kv, kv_a_norm[l]) @ w_kv_b[l]).reshape(b, s, MLA_HEADS, MLA_NOPE + MLA_V)
    k_nope = kvh[..., :MLA_NOPE]
    v = kvh[..., MLA_NOPE:]
    k_pe = _rope(k_pe[:, :, None, :], pos)[:, :, 0, :]
    y_mla = _mla_attention(q_nope, q_pe, k_nope, k_pe, v)

    x = x + jnp.concatenate([y_diff, y_mla], axis=-1) @ w_o[l]

    h = _rms(x, ffn_norm[l])
    x = x + (jax.nn.silu(h @ w_gate[l]) * (h @ w_up[l])) @ w_down[l]
    return x


def _trunk(x, attn_norm, w_in, lam_q1, lam_k1, lam_q2, lam_k2, diff_subln,
           q_a_norm, w_q_b, kv_a_norm, w_kv_b, w_o, ffn_norm, w_gate, w_up, w_down, final_norm):
    pos = jnp.arange(x.shape[1], dtype=jnp.int32)
    for l in range(DEPTH):
        x = _layer(x, pos, l, attn_norm, w_in, lam_q1, lam_k1, lam_q2, lam_k2, diff_subln,
                   q_a_norm, w_q_b, kv_a_norm, w_kv_b, w_o, ffn_norm, w_gate, w_up, w_down)
    return _rms(x, final_norm)


def setup_inputs(seed: int = 0) -> dict:
    key = jax.random.key(seed)
    ks = jax.random.split(key, 20)
    f32 = jnp.float32

    def w(k, shape, fan_in):
        return jax.random.normal(k, shape, f32) * (fan_in ** -0.5)

    def gain(k, shape):
        return 1.0 + 0.05 * jax.random.normal(k, shape, f32)

    return {
        "x_prompt": jax.random.normal(ks[0], (BATCH, SEQ, D_MODEL), f32),
        "x_sample": jax.random.normal(ks[1], (DEC_BATCH, DEC_SEQ, D_MODEL), f32),
        "attn_norm": gain(ks[2], (DEPTH, D_MODEL)),
        "w_in": w(ks[3], (DEPTH, D_MODEL, IN_COLS), D_MODEL),
        "lam_q1": 0.1 * jax.random.normal(ks[4], (DEPTH, DIFF_DH), f32),
        "lam_k1": 0.1 * jax.random.normal(ks[5], (DEPTH, DIFF_DH), f32),
        "lam_q2": 0.1 * jax.random.normal(ks[6], (DEPTH, DIFF_DH), f32),
        "lam_k2": 0.1 * jax.random.normal(ks[7], (DEPTH, DIFF_DH), f32),
        "diff_subln": gain(ks[8], (DEPTH, DIFF_DV)),
        "q_a_norm": gain(ks[9], (DEPTH, Q_LORA)),
        "w_q_b": w(ks[10], (DEPTH, Q_LORA, MLA_HEADS * (MLA_NOPE + MLA_ROPE)), Q_LORA),
        "kv_a_norm": gain(ks[11], (DEPTH, KV_LORA)),
        "w_kv_b": w(ks[12], (DEPTH, KV_LORA, MLA_HEADS * (MLA_NOPE + MLA_V)), KV_LORA),
        "w_o": w(ks[13], (DEPTH, MIX_WIDTH, D_MODEL), MIX_WIDTH),
        "ffn_norm": gain(ks[14], (DEPTH, D_MODEL)),
        "w_gate": w(ks[15], (DEPTH, D_MODEL, D_FF), D_MODEL),
        "w_up": w(ks[16], (DEPTH, D_MODEL, D_FF), D_MODEL),
        "w_down": w(ks[17], (DEPTH, D_FF, D_MODEL), D_FF),
        "final_norm": gain(ks[18], (D_MODEL,)),
    }


def reference(x_prompt, x_sample, attn_norm, w_in, lam_q1, lam_k1, lam_q2, lam_k2, diff_subln,
              q_a_norm, w_q_b, kv_a_norm, w_kv_b, w_o, ffn_norm, w_gate, w_up, w_down, final_norm):
    y_prompt = _trunk(x_prompt, attn_norm, w_in, lam_q1, lam_k1, lam_q2, lam_k2, diff_subln,
                      q_a_norm, w_q_b, kv_a_norm, w_kv_b, w_o, ffn_norm, w_gate, w_up, w_down, final_norm)
    y_sample = _trunk(x_sample, attn_norm, w_in, lam_q1, lam_k1, lam_q2, lam_k2, diff_subln,
                      q_a_norm, w_q_b, kv_a_norm, w_kv_b, w_o, ffn_norm, w_gate, w_up, w_down, final_norm)
    return (y_prompt, y_sample)
```

```python
import functools
import math

import jax
import jax.numpy as jnp
from jax import lax
from jax.experimental import pallas as pl
from jax.experimental.pallas import tpu as pltpu

D_MODEL = 1024
DEPTH = 2
DIFF_HEADS = 4
DIFF_DH = 64
DIFF_DV = 2 * DIFF_DH
DIFF_WIDTH = DIFF_HEADS * DIFF_DV
MLA_HEADS = 4
MLA_NOPE = 128
MLA_ROPE = 64
MLA_V = 128
Q_LORA = 256
KV_LORA = 128
MLA_WIDTH = MLA_HEADS * MLA_V
MIX_WIDTH = DIFF_WIDTH + MLA_WIDTH
D_FF = 2816
ROPE_THETA = 10000.0
EPS = 1e-6

LANES = 128
BF16_SUBLANES = 16
MLA_QK = 2 * LANES
V_AUG = DIFF_DV + BF16_SUBLANES
MAIN_COLS = 2 * DIFF_WIDTH + Q_LORA + KV_LORA + LANES
FF_CHUNK = 256
VMEM_LIMIT = 56 * 1024 * 1024

PROJ_TM = 512
ATTN_TQ = 256
FFN_TM = 512

_NT = (((1,), (1,)), ((), ()))
_BF = jnp.bfloat16
_F32 = jnp.float32


def _rms(x, g):
    return x * lax.rsqrt(jnp.mean(x * x, axis=-1, keepdims=True) + EPS) * g


def _proj_kernel(x_ref, g_ref, wmain_ref, wvt_ref, qg_ref, wqb_ref, kvg_ref, wkvk_ref, wkvvt_ref,
                 cos_ref, sina_ref, sinb_ref,
                 dq_ref, dk_ref, dvt_ref, mq_ref, mk_ref, mvt_ref):
    tm = x_ref.shape[1]
    h = _rms(x_ref[0], g_ref[...]).astype(_BF)
    z = jnp.dot(h, wmain_ref[...], preferred_element_type=_F32)
    cos = cos_ref[...]
    sina = sina_ref[...]
    sinb = sinb_ref[...]

    def rope(blk):
        return (blk * cos + pltpu.roll(blk, LANES - DIFF_DH // 2, 1) * sina
                + pltpu.roll(blk, DIFF_DH // 2, 1) * sinb)

    first_map = lax.broadcasted_iota(jnp.int32, (tm, LANES), 1) < DIFF_DH
    ones_rows = jnp.ones((BF16_SUBLANES, tm), _BF)
    diff_scale = DIFF_DH ** -0.5
    for hh in range(DIFF_HEADS):
        q = rope(z[:, hh * LANES:(hh + 1) * LANES]) * diff_scale
        dq_ref[0, 2 * hh] = jnp.where(first_map, q, 0.0).astype(_BF)
        dq_ref[0, 2 * hh + 1] = jnp.where(first_map, 0.0, q).astype(_BF)
        k = rope(z[:, DIFF_WIDTH + hh * LANES:DIFF_WIDTH + (hh + 1) * LANES])
        dk_ref[0, hh] = k.astype(_BF)
    vt = lax.dot_general(wvt_ref[...], h, _NT, preferred_element_type=_F32)
    for hh in range(DIFF_HEADS):
        dvt_ref[0, hh, 0:DIFF_DV, :] = vt[hh * DIFF_DV:(hh + 1) * DIFF_DV].astype(_BF)
        dvt_ref[0, hh, DIFF_DV:V_AUG, :] = ones_rows

    o3 = 2 * DIFF_WIDTH
    o4 = o3 + Q_LORA
    o5 = o4 + KV_LORA
    cq = _rms(z[:, o3:o4], qg_ref[...]).astype(_BF)
    mla_scale = (MLA_NOPE + MLA_ROPE) ** -0.5
    qh = jnp.dot(cq, wqb_ref[...], preferred_element_type=_F32) * mla_scale
    ckv = _rms(z[:, o4:o5], kvg_ref[...]).astype(_BF)
    kn = jnp.dot(ckv, wkvk_ref[...], preferred_element_type=_F32)
    kpe = rope(z[:, o5:o5 + LANES]).astype(_BF)
    mvt = lax.dot_general(wkvvt_ref[...], ckv, _NT, preferred_element_type=_F32)
    for hh in range(MLA_HEADS):
        base = hh * MLA_QK
        mq_ref[0, hh, :, 0:LANES] = qh[:, base:base + LANES].astype(_BF)
        mq_ref[0, hh, :, LANES:MLA_QK] = rope(qh[:, base + LANES:base + MLA_QK]).astype(_BF)
        mk_ref[0, hh, :, 0:LANES] = kn[:, hh * MLA_NOPE:(hh + 1) * MLA_NOPE].astype(_BF)
        mk_ref[0, hh, :, LANES:MLA_QK] = kpe
        mvt_ref[0, hh, 0:MLA_V, :] = mvt[hh * MLA_V:(hh + 1) * MLA_V].astype(_BF)
        mvt_ref[0, hh, MLA_V:V_AUG, :] = ones_rows


def _attn_kernel(x_ref, dq_ref, dk_ref, dvt_ref, mq_ref, mk_ref, mvt_ref, wot_ref, sg_ref, lam_ref,
                 out_ref, mix_ref, *, lam_init):
    tq = x_ref.shape[1]
    lam = (jnp.exp(jnp.sum(lam_ref[0:1, :] * lam_ref[1:2, :], keepdims=True))
           - jnp.exp(jnp.sum(lam_ref[2:3, :] * lam_ref[3:4, :], keepdims=True))
           + lam_init)

    def probs_t(k, q):
        st = lax.dot_general(k, q, _NT, preferred_element_type=_F32)
        m = jnp.max(st, axis=0, keepdims=True)
        return jnp.exp(st - m).astype(_BF)

    def diff_body(hh, carry):
        k = dk_ref[0, hh]
        p = jnp.concatenate([probs_t(k, dq_ref[0, 2 * hh]), probs_t(k, dq_ref[0, 2 * hh + 1])], axis=1)
        oa = jnp.dot(dvt_ref[0, hh], p, preferred_element_type=_F32)
        o1 = oa[0:DIFF_DV, 0:tq] * (1.0 / oa[DIFF_DV:DIFF_DV + 1, 0:tq])
        o2 = oa[0:DIFF_DV, tq:2 * tq] * (1.0 / oa[DIFF_DV:DIFF_DV + 1, tq:2 * tq])
        d = o1 - lam * o2
        ms = jnp.mean(d * d, axis=0, keepdims=True)
        y = d * lax.rsqrt(ms + EPS) * sg_ref[...] * (1.0 - lam_init)
        mix_ref[hh] = y.astype(_BF)
        return carry

    lax.fori_loop(0, DIFF_HEADS, diff_body, 0)

    def mla_body(hh, carry):
        p = probs_t(mk_ref[0, hh], mq_ref[0, hh])
        oa = jnp.dot(mvt_ref[0, hh], p, preferred_element_type=_F32)
        o = oa[0:MLA_V] * (1.0 / oa[MLA_V:MLA_V + 1])
        mix_ref[DIFF_HEADS + hh] = o.astype(_BF)
        return carry

    lax.fori_loop(0, MLA_HEADS, mla_body, 0)

    mix_t = mix_ref[...].reshape(MIX_WIDTH, tq)
    yt = jnp.dot(wot_ref[...], mix_t, preferred_element_type=_F32)
    out_ref[0] = x_ref[0] + yt.T


def _ffn_kernel(x_ref, g_ref, wg_ref, wu_ref, wd_ref, fg_ref, out_ref, act_ref, *, final):
    x = x_ref[0]
    h = _rms(x, g_ref[...]).astype(_BF)
    for c in range(D_FF // FF_CHUNK):
        cols = slice(c * FF_CHUNK, (c + 1) * FF_CHUNK)
        gate = jnp.dot(h, wg_ref[:, cols], preferred_element_type=_F32)
        up = jnp.dot(h, wu_ref[:, cols], preferred_element_type=_F32)
        act_ref[:, cols] = (gate * (1.0 / (1.0 + jnp.exp(-gate))) * up).astype(_BF)
    y = x + jnp.dot(act_ref[...], wd_ref[...], preferred_element_type=_F32)
    if final:
        y = _rms(y, fg_ref[...])
    out_ref[0] = y


def _const_spec(shape):
    zeros = (0,) * len(shape)
    return pl.BlockSpec(shape, lambda b, i: zeros, pipeline_mode=pl.Buffered(1))


def _params():
    return pltpu.CompilerParams(dimension_semantics=("arbitrary", "arbitrary"),
                                vmem_limit_bytes=VMEM_LIMIT)


def _proj_call(x, w, tabs):
    bsz, seq, _ = x.shape
    tm = min(PROJ_TM, seq)
    grid = (bsz, seq // tm)
    tok = lambda width: pl.BlockSpec((1, tm, width), lambda b, i: (b, i, 0))
    head_tok = lambda nh, width: pl.BlockSpec((1, nh, tm, width), lambda b, i: (b, 0, i, 0))
    head_feat = lambda nh: pl.BlockSpec((1, nh, V_AUG, tm), lambda b, i: (b, 0, 0, i))
    tab = pl.BlockSpec((tm, LANES), lambda b, i: (i, 0))
    consts = [w["attn_g"], w["w_main"], w["w_vt"], w["q_g"], w["w_qb"], w["kv_g"], w["w_kvk"], w["w_kvvt"]]
    return pl.pallas_call(
        _proj_kernel,
        grid=grid,
        in_specs=[tok(D_MODEL)] + [_const_spec(c.shape) for c in consts] + [tab, tab, tab],
        out_specs=[head_tok(2 * DIFF_HEADS, LANES), head_tok(DIFF_HEADS, LANES), head_feat(DIFF_HEADS),
                   head_tok(MLA_HEADS, MLA_QK), head_tok(MLA_HEADS, MLA_QK), head_feat(MLA_HEADS)],
        out_shape=[jax.ShapeDtypeStruct((bsz, 2 * DIFF_HEADS, seq, LANES), _BF),
                   jax.ShapeDtypeStruct((bsz, DIFF_HEADS, seq, LANES), _BF),
                   jax.ShapeDtypeStruct((bsz, DIFF_HEADS, V_AUG, seq), _BF),
                   jax.ShapeDtypeStruct((bsz, MLA_HEADS, seq, MLA_QK), _BF),
                   jax.ShapeDtypeStruct((bsz, MLA_HEADS, seq, MLA_QK), _BF),
                   jax.ShapeDtypeStruct((bsz, MLA_HEADS, V_AUG, seq), _BF)],
        compiler_params=_params(),
        name="proj",
    )(x, *consts, *tabs)


def _attn_call(x, qkv, w, lam_init):
    bsz, seq, _ = x.shape
    tq = min(ATTN_TQ, seq)
    grid = (bsz, seq // tq)
    dq, dk, dvt, mq, mk, mvt = qkv
    tok = pl.BlockSpec((1, tq, D_MODEL), lambda b, i: (b, i, 0))
    q_spec = lambda nh, width: pl.BlockSpec((1, nh, tq, width), lambda b, i: (b, 0, i, 0))
    full = lambda a: pl.BlockSpec((1,) + a.shape[1:], lambda b, i: (b, 0, 0, 0))
    consts = [w["w_ot"], w["subln_g"], w["lam"]]
    return pl.pallas_call(
        functools.partial(_attn_kernel, lam_init=lam_init),
        grid=grid,
        in_specs=[tok, q_spec(2 * DIFF_HEADS, LANES), full(dk), full(dvt),
                  q_spec(MLA_HEADS, MLA_QK), full(mk), full(mvt)] + [_const_spec(c.shape) for c in consts],
        out_specs=tok,
        out_shape=jax.ShapeDtypeStruct(x.shape, _F32),
        scratch_shapes=[pltpu.VMEM((DIFF_HEADS + MLA_HEADS, DIFF_DV, tq), _BF)],
        compiler_params=_params(),
        name="attn",
    )(x, dq, dk, dvt, mq, mk, mvt, *consts)


def _ffn_call(x, w, final_g, final):
    bsz, seq, _ = x.shape
    tm = min(FFN_TM, seq)
    grid = (bsz, seq // tm)
    tok = pl.BlockSpec((1, tm, D_MODEL), lambda b, i: (b, i, 0))
    consts = [w["ffn_g"], w["w_gate"], w["w_up"], w["w_down"], final_g]
    return pl.pallas_call(
        functools.partial(_ffn_kernel, final=final),
        grid=grid,
        in_specs=[tok] + [_const_spec(c.shape) for c in consts],
        out_specs=tok,
        out_shape=jax.ShapeDtypeStruct(x.shape, _F32),
        scratch_shapes=[pltpu.VMEM((tm, D_FF), _BF)],
        compiler_params=_params(),
        name="ffn",
    )(x, *consts)


def _rope_tables(seq):
    half = DIFF_DH // 2
    inv = 1.0 / (ROPE_THETA ** (jnp.arange(half, dtype=_F32) * 2.0 / DIFF_DH))
    ang = jnp.arange(seq, dtype=jnp.int32).astype(_F32)[:, None] * inv[None, :]
    cos, sin, zero = jnp.cos(ang), jnp.sin(ang), jnp.zeros_like(ang)
    reps = LANES // DIFF_DH
    cos_t = jnp.tile(jnp.concatenate([cos, cos], axis=-1), (1, reps))
    sina_t = jnp.tile(jnp.concatenate([-sin, zero], axis=-1), (1, reps))
    sinb_t = jnp.tile(jnp.concatenate([zero, sin], axis=-1), (1, reps))
    return cos_t, sina_t, sinb_t


def _layer_weights(l, attn_norm, w_in, lam_q1, lam_k1, lam_q2, lam_k2, diff_subln, q_a_norm, w_q_b,
                   kv_a_norm, w_kv_b, w_o, ffn_norm, w_gate, w_up, w_down):
    o1 = DIFF_WIDTH
    o2 = o1 + DIFF_WIDTH
    o3 = o2 + DIFF_WIDTH
    wi = w_in[l]
    pad = jnp.zeros((D_MODEL, LANES - MLA_ROPE), _F32)
    w_main = jnp.concatenate([wi[:, :o2], wi[:, o3:], pad], axis=1).astype(_BF)
    w_vt = wi[:, o2:o3].T.astype(_BF)
    qb = w_q_b[l].reshape(Q_LORA, MLA_HEADS, MLA_NOPE + MLA_ROPE)
    qb = jnp.pad(qb, ((0, 0), (0, 0), (0, MLA_QK - MLA_NOPE - MLA_ROPE)))
    kvb = w_kv_b[l].reshape(KV_LORA, MLA_HEADS, MLA_NOPE + MLA_V)
    return {
        "attn_g": attn_norm[l][None, :],
        "w_main": w_main,
        "w_vt": w_vt,
        "q_g": q_a_norm[l][None, :],
        "w_qb": qb.reshape(Q_LORA, MLA_HEADS * MLA_QK).astype(_BF),
        "kv_g": kv_a_norm[l][None, :],
        "w_kvk": kvb[:, :, :MLA_NOPE].reshape(KV_LORA, MLA_HEADS * MLA_NOPE).astype(_BF),
        "w_kvvt": kvb[:, :, MLA_NOPE:].reshape(KV_LORA, MLA_WIDTH).T.astype(_BF),
        "w_ot": w_o[l].T.astype(_BF),
        "subln_g": diff_subln[l][:, None],
        "lam": jnp.stack([lam_q1[l], lam_k1[l], lam_q2[l], lam_k2[l]]),
        "ffn_g": ffn_norm[l][None, :],
        "w_gate": w_gate[l].astype(_BF),
        "w_up": w_up[l].astype(_BF),
        "w_down": w_down[l].astype(_BF),
    }


def _trunk(x, layers, final_g, tabs):
    for l, w in enumerate(layers):
        lam_init = 0.8 - 0.6 * math.exp(-0.3 * l)
        qkv = _proj_call(x, w, tabs)
        x = _attn_call(x, qkv, w, lam_init)
        x = _ffn_call(x, w, final_g, final=(l == len(layers) - 1))
    return x


def kernel(x_prompt, x_sample, attn_norm, w_in, lam_q1, lam_k1, lam_q2, lam_k2, diff_subln, q_a_norm, w_q_b,
           kv_a_norm, w_kv_b, w_o, ffn_norm, w_gate, w_up, w_down, final_norm):
    layers = [_layer_weights(l, attn_norm, w_in, lam_q1, lam_k1, lam_q2, lam_k2, diff_subln, q_a_norm,
                             w_q_b, kv_a_norm, w_kv_b, w_o, ffn_norm, w_gate, w_up, w_down)
              for l in range(DEPTH)]
    final_g = final_norm[None, :]
    outs = []
    for x in (x_prompt, x_sample):
        outs.append(_trunk(x, layers, final_g, _rope_tables(x.shape[1])))
    return tuple(outs)
```

```python
import functools
import math

import jax
import jax.numpy as jnp
from jax import lax
from jax.experimental import pallas as pl
from jax.experimental.pallas import tpu as pltpu

D_MODEL = 1024
DEPTH = 2
DIFF_HEADS = 4
DIFF_DH = 64
DIFF_DV = 2 * DIFF_DH
DIFF_WIDTH = DIFF_HEADS * DIFF_DV
MLA_HEADS = 4
MLA_NOPE = 128
MLA_ROPE = 64
MLA_V = 128
Q_LORA = 256
KV_LORA = 128
MLA_WIDTH = MLA_HEADS * MLA_V
MIX_WIDTH = DIFF_WIDTH + MLA_WIDTH
D_FF = 2816
ROPE_THETA = 10000.0
EPS = 1e-6

LANES = 128
BF16_SUBLANES = 16
MLA_QK = 2 * LANES
V_AUG = DIFF_DV + BF16_SUBLANES
MAIN_COLS = 2 * DIFF_WIDTH + Q_LORA + KV_LORA + LANES
FF_CHUNK = 256
KEY_CHUNK = 512
LOG2_E = math.log2(math.e)
VMEM_LIMIT = 56 * 1024 * 1024

PROJ_TM = 512
ATTN_TQ = 256
FFN_TM = 512

_NT = (((1,), (1,)), ((), ()))
_BF = jnp.bfloat16
_F32 = jnp.float32


def _rms(x, g):
    return x * lax.rsqrt(jnp.mean(x * x, axis=-1, keepdims=True) + EPS) * g


def _proj_kernel(x_ref, g_ref, wmain_ref, wvt_ref, qg_ref, wqb_ref, kvg_ref, wkvk_ref, wkvvt_ref,
                 cos_ref, sina_ref, sinb_ref,
                 dq_ref, dk_ref, dvt_ref, mq_ref, mk_ref, mvt_ref):
    tm = x_ref.shape[1]
    h = _rms(x_ref[0], g_ref[...]).astype(_BF)
    z = jnp.dot(h, wmain_ref[...], preferred_element_type=_F32)
    cos = cos_ref[...]
    sina = sina_ref[...]
    sinb = sinb_ref[...]

    def rope(blk):
        return (blk * cos + pltpu.roll(blk, LANES - DIFF_DH // 2, 1) * sina
                + pltpu.roll(blk, DIFF_DH // 2, 1) * sinb)

    first_map = lax.broadcasted_iota(jnp.int32, (tm, LANES), 1) < DIFF_DH
    ones_rows = jnp.ones((BF16_SUBLANES, tm), _BF)
    diff_scale = DIFF_DH ** -0.5 * LOG2_E
    for hh in range(DIFF_HEADS):
        q = rope(z[:, hh * LANES:(hh + 1) * LANES]) * diff_scale
        dq_ref[0, 2 * hh] = jnp.where(first_map, q, 0.0).astype(_BF)
        dq_ref[0, 2 * hh + 1] = jnp.where(first_map, 0.0, q).astype(_BF)
        k = rope(z[:, DIFF_WIDTH + hh * LANES:DIFF_WIDTH + (hh + 1) * LANES])
        dk_ref[0, hh] = k.astype(_BF)
    vt = lax.dot_general(wvt_ref[...], h, _NT, preferred_element_type=_F32)
    for hh in range(DIFF_HEADS):
        dvt_ref[0, hh, 0:DIFF_DV, :] = vt[hh * DIFF_DV:(hh + 1) * DIFF_DV].astype(_BF)
        dvt_ref[0, hh, DIFF_DV:V_AUG, :] = ones_rows

    o3 = 2 * DIFF_WIDTH
    o4 = o3 + Q_LORA
    o5 = o4 + KV_LORA
    cq = _rms(z[:, o3:o4], qg_ref[...]).astype(_BF)
    mla_scale = (MLA_NOPE + MLA_ROPE) ** -0.5 * LOG2_E
    qh = jnp.dot(cq, wqb_ref[...], preferred_element_type=_F32) * mla_scale
    ckv = _rms(z[:, o4:o5], kvg_ref[...]).astype(_BF)
    kn = jnp.dot(ckv, wkvk_ref[...], preferred_element_type=_F32)
    kpe = rope(z[:, o5:o5 + LANES]).astype(_BF)
    mvt = lax.dot_general(wkvvt_ref[...], ckv, _NT, preferred_element_type=_F32)
    for hh in range(MLA_HEADS):
        base = hh * MLA_QK
        mq_ref[0, hh, :, 0:LANES] = qh[:, base:base + LANES].astype(_BF)
        mq_ref[0, hh, :, LANES:MLA_QK] = rope(qh[:, base + LANES:base + MLA_QK]).astype(_BF)
        mk_ref[0, hh, :, 0:LANES] = kn[:, hh * MLA_NOPE:(hh + 1) * MLA_NOPE].astype(_BF)
        mk_ref[0, hh, :, LANES:MLA_QK] = kpe
        mvt_ref[0, hh, 0:MLA_V, :] = mvt[hh * MLA_V:(hh + 1) * MLA_V].astype(_BF)
        mvt_ref[0, hh, MLA_V:V_AUG, :] = ones_rows


def _attn_kernel(x_ref, dq_ref, dk_ref, dvt_ref, mq_ref, mk_ref, mvt_ref, wot_ref, sg_ref, lam_ref,
                 out_ref, st_ref, p_ref, mix_ref, *, lam_init):
    tq = x_ref.shape[1]
    seq = dk_ref.shape[2]
    n_chunks = seq // KEY_CHUNK
    lam = (jnp.exp(jnp.sum(lam_ref[0:1, :] * lam_ref[1:2, :], keepdims=True))
           - jnp.exp(jnp.sum(lam_ref[2:3, :] * lam_ref[3:4, :], keepdims=True))
           + lam_init)

    maps = []
    for hh in range(DIFF_HEADS):
        maps.append((dk_ref, hh, dq_ref, 2 * hh, hh, 0))
        maps.append((dk_ref, hh, dq_ref, 2 * hh + 1, hh, 1))
    for hh in range(MLA_HEADS):
        maps.append((mk_ref, hh, mq_ref, hh, DIFF_HEADS + hh, 0))

    def score(i):
        k_ref, ki, q_ref, qi, _, _ = maps[i]
        q = q_ref[0, qi]
        m = None
        for c in range(n_chunks):
            rows = slice(c * KEY_CHUNK, (c + 1) * KEY_CHUNK)
            st = lax.dot_general(k_ref[0, ki, rows, :], q, _NT, preferred_element_type=_F32)
            st_ref[i % 2, rows, :] = st
            mc = jnp.max(st, axis=0, keepdims=True)
            m = mc if m is None else jnp.maximum(m, mc)
        return m

    def probs(i, m):
        _, _, _, _, g, col = maps[i]
        for c in range(n_chunks):
            rows = slice(c * KEY_CHUNK, (c + 1) * KEY_CHUNK)
            p_ref[g % 2, rows, col * tq:(col + 1) * tq] = jnp.exp2(st_ref[i % 2, rows, :] - m).astype(_BF)

    def mix(g):
        if g < DIFF_HEADS:
            oa = jnp.dot(dvt_ref[0, g], p_ref[g % 2], preferred_element_type=_F32)
            o1 = oa[0:DIFF_DV, 0:tq] * (1.0 / oa[DIFF_DV:DIFF_DV + 1, 0:tq])
            o2 = oa[0:DIFF_DV, tq:2 * tq] * (1.0 / oa[DIFF_DV:DIFF_DV + 1, tq:2 * tq])
            d = o1 - lam * o2
            ms = jnp.mean(d * d, axis=0, keepdims=True)
            y = d * lax.rsqrt(ms + EPS) * sg_ref[...] * (1.0 - lam_init)
            mix_ref[g] = y.astype(_BF)
        else:
            oa = jnp.dot(mvt_ref[0, g - DIFF_HEADS], p_ref[g % 2, :, 0:tq], preferred_element_type=_F32)
            mix_ref[g] = (oa[0:MLA_V] * (1.0 / oa[MLA_V:MLA_V + 1])).astype(_BF)

    n_maps = len(maps)
    m_prev = score(0)
    for i in range(1, n_maps + 1):
        m_next = score(i) if i < n_maps else None
        probs(i - 1, m_prev)
        if i == n_maps or maps[i][4] != maps[i - 1][4]:
            mix(maps[i - 1][4])
        m_prev = m_next

    mix_t = mix_ref[...].reshape(MIX_WIDTH, tq)
    yt = jnp.dot(wot_ref[...], mix_t, preferred_element_type=_F32)
    out_ref[0] = x_ref[0] + yt.T


def _ffn_kernel(x_ref, g_ref, wg_ref, wu_ref, wd_ref, fg_ref, out_ref, act_ref, *, final):
    x = x_ref[0]
    h = _rms(x, g_ref[...]).astype(_BF)
    for c in range(D_FF // FF_CHUNK):
        cols = slice(c * FF_CHUNK, (c + 1) * FF_CHUNK)
        gate = jnp.dot(h, wg_ref[:, cols], preferred_element_type=_F32)
        up = jnp.dot(h, wu_ref[:, cols], preferred_element_type=_F32)
        act_ref[:, cols] = (gate * (1.0 / (1.0 + jnp.exp(-gate))) * up).astype(_BF)
    y = x + jnp.dot(act_ref[...], wd_ref[...], preferred_element_type=_F32)
    if final:
        y = _rms(y, fg_ref[...])
    out_ref[0] = y


def _const_spec(shape):
    zeros = (0,) * len(shape)
    return pl.BlockSpec(shape, lambda b, i: zeros, pipeline_mode=pl.Buffered(1))


def _params():
    return pltpu.CompilerParams(dimension_semantics=("arbitrary", "arbitrary"),
                                vmem_limit_bytes=VMEM_LIMIT)


def _proj_call(x, w, tabs):
    bsz, seq, _ = x.shape
    tm = min(PROJ_TM, seq)
    grid = (bsz, seq // tm)
    tok = lambda width: pl.BlockSpec((1, tm, width), lambda b, i: (b, i, 0))
    head_tok = lambda nh, width: pl.BlockSpec((1, nh, tm, width), lambda b, i: (b, 0, i, 0))
    head_feat = lambda nh: pl.BlockSpec((1, nh, V_AUG, tm), lambda b, i: (b, 0, 0, i))
    tab = pl.BlockSpec((tm, LANES), lambda b, i: (i, 0))
    consts = [w["attn_g"], w["w_main"], w["w_vt"], w["q_g"], w["w_qb"], w["kv_g"], w["w_kvk"], w["w_kvvt"]]
    return pl.pallas_call(
        _proj_kernel,
        grid=grid,
        in_specs=[tok(D_MODEL)] + [_const_spec(c.shape) for c in consts] + [tab, tab, tab],
        out_specs=[head_tok(2 * DIFF_HEADS, LANES), head_tok(DIFF_HEADS, LANES), head_feat(DIFF_HEADS),
                   head_tok(MLA_HEADS, MLA_QK), head_tok(MLA_HEADS, MLA_QK), head_feat(MLA_HEADS)],
        out_shape=[jax.ShapeDtypeStruct((bsz, 2 * DIFF_HEADS, seq, LANES), _BF),
                   jax.ShapeDtypeStruct((bsz, DIFF_HEADS, seq, LANES), _BF),
                   jax.ShapeDtypeStruct((bsz, DIFF_HEADS, V_AUG, seq), _BF),
                   jax.ShapeDtypeStruct((bsz, MLA_HEADS, seq, MLA_QK), _BF),
                   jax.ShapeDtypeStruct((bsz, MLA_HEADS, seq, MLA_QK), _BF),
                   jax.ShapeDtypeStruct((bsz, MLA_HEADS, V_AUG, seq), _BF)],
        compiler_params=_params(),
        name="proj",
    )(x, *consts, *tabs)


def _attn_call(x, qkv, w, lam_init):
    bsz, seq, _ = x.shape
    tq = min(ATTN_TQ, seq)
    grid = (bsz, seq // tq)
    dq, dk, dvt, mq, mk, mvt = qkv
    tok = pl.BlockSpec((1, tq, D_MODEL), lambda b, i: (b, i, 0))
    q_spec = lambda nh, width: pl.BlockSpec((1, nh, tq, width), lambda b, i: (b, 0, i, 0))
    full = lambda a: pl.BlockSpec((1,) + a.shape[1:], lambda b, i: (b, 0, 0, 0))
    consts = [w["w_ot"], w["subln_g"], w["lam"]]
    return pl.pallas_call(
        functools.partial(_attn_kernel, lam_init=lam_init),
        grid=grid,
        in_specs=[tok, q_spec(2 * DIFF_HEADS, LANES), full(dk), full(dvt),
                  q_spec(MLA_HEADS, MLA_QK), full(mk), full(mvt)] + [_const_spec(c.shape) for c in consts],
        out_specs=tok,
        out_shape=jax.ShapeDtypeStruct(x.shape, _F32),
        scratch_shapes=[pltpu.VMEM((2, seq, tq), _F32),
                        pltpu.VMEM((2, seq, 2 * tq), _BF),
                        pltpu.VMEM((DIFF_HEADS + MLA_HEADS, DIFF_DV, tq), _BF)],
        compiler_params=_params(),
        name="attn",
    )(x, dq, dk, dvt, mq, mk, mvt, *consts)


def _ffn_call(x, w, final_g, final):
    bsz, seq, _ = x.shape
    tm = min(FFN_TM, seq)
    grid = (bsz, seq // tm)
    tok = pl.BlockSpec((1, tm, D_MODEL), lambda b, i: (b, i, 0))
    consts = [w["ffn_g"], w["w_gate"], w["w_up"], w["w_down"], final_g]
    return pl.pallas_call(
        functools.partial(_ffn_kernel, final=final),
        grid=grid,
        in_specs=[tok] + [_const_spec(c.shape) for c in consts],
        out_specs=tok,
        out_shape=jax.ShapeDtypeStruct(x.shape, _F32),
        scratch_shapes=[pltpu.VMEM((tm, D_FF), _BF)],
        compiler_params=_params(),
        name="ffn",
    )(x, *consts)


def _rope_tables(seq):
    half = DIFF_DH // 2
    inv = 1.0 / (ROPE_THETA ** (jnp.arange(half, dtype=_F32) * 2.0 / DIFF_DH))
    ang = jnp.arange(seq, dtype=jnp.int32).astype(_F32)[:, None] * inv[None, :]
    cos, sin, zero = jnp.cos(ang), jnp.sin(ang), jnp.zeros_like(ang)
    reps = LANES // DIFF_DH
    cos_t = jnp.tile(jnp.concatenate([cos, cos], axis=-1), (1, reps))
    sina_t = jnp.tile(jnp.concatenate([-sin, zero], axis=-1), (1, reps))
    sinb_t = jnp.tile(jnp.concatenate([zero, sin], axis=-1), (1, reps))
    return cos_t, sina_t, sinb_t


def _layer_weights(l, attn_norm, w_in, lam_q1, lam_k1, lam_q2, lam_k2, diff_subln, q_a_norm, w_q_b,
                   kv_a_norm, w_kv_b, w_o, ffn_norm, w_gate, w_up, w_down):
    o1 = DIFF_WIDTH
    o2 = o1 + DIFF_WIDTH
    o3 = o2 + DIFF_WIDTH
    wi = w_in[l]
    pad = jnp.zeros((D_MODEL, LANES - MLA_ROPE), _F32)
    w_main = jnp.concatenate([wi[:, :o2], wi[:, o3:], pad], axis=1).astype(_BF)
    w_vt = wi[:, o2:o3].T.astype(_BF)
    qb = w_q_b[l].reshape(Q_LORA, MLA_HEADS, MLA_NOPE + MLA_ROPE)
    qb = jnp.pad(qb, ((0, 0), (0, 0), (0, MLA_QK - MLA_NOPE - MLA_ROPE)))
    kvb = w_kv_b[l].reshape(KV_LORA, MLA_HEADS, MLA_NOPE + MLA_V)
    return {
        "attn_g": attn_norm[l][None, :],
        "w_main": w_main,
        "w_vt": w_vt,
        "q_g": q_a_norm[l][None, :],
        "w_qb": qb.reshape(Q_LORA, MLA_HEADS * MLA_QK).astype(_BF),
        "kv_g": kv_a_norm[l][None, :],
        "w_kvk": kvb[:, :, :MLA_NOPE].reshape(KV_LORA, MLA_HEADS * MLA_NOPE).astype(_BF),
        "w_kvvt": kvb[:, :, MLA_NOPE:].reshape(KV_LORA, MLA_WIDTH).T.astype(_BF),
        "w_ot": w_o[l].T.astype(_BF),
        "subln_g": diff_subln[l][:, None],
        "lam": jnp.stack([lam_q1[l], lam_k1[l], lam_q2[l], lam_k2[l]]),
        "ffn_g": ffn_norm[l][None, :],
        "w_gate": w_gate[l].astype(_BF),
        "w_up": w_up[l].astype(_BF),
        "w_down": w_down[l].astype(_BF),
    }


def _trunk(x, layers, final_g, tabs):
    for l, w in enumerate(layers):
        lam_init = 0.8 - 0.6 * math.exp(-0.3 * l)
        qkv = _proj_call(x, w, tabs)
        x = _attn_call(x, qkv, w, lam_init)
        x = _ffn_call(x, w, final_g, final=(l == len(layers) - 1))
    return x


def kernel(x_prompt, x_sample, attn_norm, w_in, lam_q1, lam_k1, lam_q2, lam_k2, diff_subln, q_a_norm, w_q_b,
           kv_a_norm, w_kv_b, w_o, ffn_norm, w_gate, w_up, w_down, final_norm):
    layers = [_layer_weights(l, attn_norm, w_in, lam_q1, lam_k1, lam_q2, lam_k2, diff_subln, q_a_norm,
                             w_q_b, kv_a_norm, w_kv_b, w_o, ffn_norm, w_gate, w_up, w_down)
              for l in range(DEPTH)]
    final_g = final_norm[None, :]
    outs = []
    for x in (x_prompt, x_sample):
        outs.append(_trunk(x, layers, final_g, _rope_tables(x.shape[1])))
    return tuple(outs)
```

```python
import functools
import math

import jax
import jax.numpy as jnp
from jax import lax
from jax.experimental import pallas as pl
from jax.experimental.pallas import tpu as pltpu

D_MODEL = 1024
DEPTH = 2
DIFF_HEADS = 4
DIFF_DH = 64
DIFF_DV = 2 * DIFF_DH
DIFF_WIDTH = DIFF_HEADS * DIFF_DV
MLA_HEADS = 4
MLA_NOPE = 128
MLA_ROPE = 64
MLA_V = 128
Q_LORA = 256
KV_LORA = 128
MLA_WIDTH = MLA_HEADS * MLA_V
MIX_WIDTH = DIFF_WIDTH + MLA_WIDTH
D_FF = 2816
ROPE_THETA = 10000.0
EPS = 1e-6

LANES = 128
BF16_SUBLANES = 16
MLA_QK = 2 * LANES
V_AUG = DIFF_DV + BF16_SUBLANES
MAIN_COLS = 2 * DIFF_WIDTH + Q_LORA + KV_LORA + LANES
FF_CHUNK = 256
N_MAPS = 2 * DIFF_HEADS + MLA_HEADS
KEY_CHUNK = 512
LOG2_E = math.log2(math.e)
VMEM_LIMIT = 56 * 1024 * 1024

PROJ_TM = 512
ATTN_TQ = 256
FFN_TM = 512

_NT = (((1,), (1,)), ((), ()))
_BF = jnp.bfloat16
_F32 = jnp.float32


def _rms(x, g):
    return x * lax.rsqrt(jnp.mean(x * x, axis=-1, keepdims=True) + EPS) * g


def _proj_kernel(x_ref, g_ref, wmain_ref, wvt_ref, qg_ref, wqb_ref, kvg_ref, wkvk_ref, wkvvt_ref,
                 cos_ref, sina_ref, sinb_ref,
                 dq_ref, dk_ref, dvt_ref, mq_ref, mk_ref, mvt_ref):
    tm = x_ref.shape[1]
    h = _rms(x_ref[0], g_ref[...]).astype(_BF)
    z = jnp.dot(h, wmain_ref[...], preferred_element_type=_F32)
    cos = cos_ref[...]
    sina = sina_ref[...]
    sinb = sinb_ref[...]

    def rope(blk):
        return (blk * cos + pltpu.roll(blk, LANES - DIFF_DH // 2, 1) * sina
                + pltpu.roll(blk, DIFF_DH // 2, 1) * sinb)

    first_map = lax.broadcasted_iota(jnp.int32, (tm, LANES), 1) < DIFF_DH
    ones_rows = jnp.ones((BF16_SUBLANES, tm), _BF)
    diff_scale = DIFF_DH ** -0.5 * LOG2_E
    for hh in range(DIFF_HEADS):
        q = rope(z[:, hh * LANES:(hh + 1) * LANES]) * diff_scale
        dq_ref[0, 2 * hh] = jnp.where(first_map, q, 0.0).astype(_BF)
        dq_ref[0, 2 * hh + 1] = jnp.where(first_map, 0.0, q).astype(_BF)
        k = rope(z[:, DIFF_WIDTH + hh * LANES:DIFF_WIDTH + (hh + 1) * LANES])
        dk_ref[0, hh] = k.astype(_BF)
    vt = lax.dot_general(wvt_ref[...], h, _NT, preferred_element_type=_F32)
    for hh in range(DIFF_HEADS):
        dvt_ref[0, hh, 0:DIFF_DV, :] = vt[hh * DIFF_DV:(hh + 1) * DIFF_DV].astype(_BF)
        dvt_ref[0, hh, DIFF_DV:V_AUG, :] = ones_rows

    o3 = 2 * DIFF_WIDTH
    o4 = o3 + Q_LORA
    o5 = o4 + KV_LORA
    cq = _rms(z[:, o3:o4], qg_ref[...]).astype(_BF)
    mla_scale = (MLA_NOPE + MLA_ROPE) ** -0.5 * LOG2_E
    qh = jnp.dot(cq, wqb_ref[...], preferred_element_type=_F32) * mla_scale
    ckv = _rms(z[:, o4:o5], kvg_ref[...]).astype(_BF)
    kn = jnp.dot(ckv, wkvk_ref[...], preferred_element_type=_F32)
    kpe = rope(z[:, o5:o5 + LANES]).astype(_BF)
    mvt = lax.dot_general(wkvvt_ref[...], ckv, _NT, preferred_element_type=_F32)
    for hh in range(MLA_HEADS):
        base = hh * MLA_QK
        mq_ref[0, hh, :, 0:LANES] = qh[:, base:base + LANES].astype(_BF)
        mq_ref[0, hh, :, LANES:MLA_QK] = rope(qh[:, base + LANES:base + MLA_QK]).astype(_BF)
        mk_ref[0, hh, :, 0:LANES] = kn[:, hh * MLA_NOPE:(hh + 1) * MLA_NOPE].astype(_BF)
        mk_ref[0, hh, :, LANES:MLA_QK] = kpe
        mvt_ref[0, hh, 0:MLA_V, :] = mvt[hh * MLA_V:(hh + 1) * MLA_V].astype(_BF)
        mvt_ref[0, hh, MLA_V:V_AUG, :] = ones_rows


def _attn_kernel(x_ref, dq_ref, dk_ref, dvt_ref, mq_ref, mk_ref, mvt_ref, mvt_fin_ref, wot_ref, sg_ref, lam_ref,
                 out_ref, st_ref, p_ref, mix_ref, m_ref, *, lam_init):
    tq = x_ref.shape[1]
    seq = dk_ref.shape[2]
    n_chunks = seq // KEY_CHUNK
    last = N_MAPS - 1

    @pl.when(pl.program_id(0) == 0)
    def _():
        mix_ref[...] = jnp.zeros_like(mix_ref)
        st_ref[last % 2] = jnp.zeros((seq, tq), _F32)
        m_ref[...] = jnp.zeros_like(m_ref)

    lam = (jnp.exp(jnp.sum(lam_ref[0:1, :] * lam_ref[1:2, :], keepdims=True))
           - jnp.exp(jnp.sum(lam_ref[2:3, :] * lam_ref[3:4, :], keepdims=True))
           + lam_init)

    maps = []
    for hh in range(DIFF_HEADS):
        maps.append((dk_ref, hh, dq_ref, 2 * hh, hh, 0))
        maps.append((dk_ref, hh, dq_ref, 2 * hh + 1, hh, 1))
    for hh in range(MLA_HEADS):
        maps.append((mk_ref, hh, mq_ref, hh, DIFF_HEADS + hh, 0))

    def score(i):
        k_ref, ki, q_ref, qi, _, _ = maps[i]
        q = q_ref[0, qi]
        m = None
        for c in range(n_chunks):
            rows = slice(c * KEY_CHUNK, (c + 1) * KEY_CHUNK)
            st = lax.dot_general(k_ref[0, ki, rows, :], q, _NT, preferred_element_type=_F32)
            st_ref[i % 2, rows, :] = st
            mc = jnp.max(st, axis=0, keepdims=True)
            m = mc if m is None else jnp.maximum(m, mc)
        return m

    def probs(i, m):
        _, _, _, _, g, col = maps[i]
        for c in range(n_chunks):
            rows = slice(c * KEY_CHUNK, (c + 1) * KEY_CHUNK)
            p_ref[g % 2, rows, col * tq:(col + 1) * tq] = jnp.exp2(st_ref[i % 2, rows, :] - m).astype(_BF)

    def mix(g, vt_ref, vi):
        if g < DIFF_HEADS:
            oa = jnp.dot(vt_ref[0, vi], p_ref[g % 2], preferred_element_type=_F32)
            o1 = oa[0:DIFF_DV, 0:tq] * (1.0 / oa[DIFF_DV:DIFF_DV + 1, 0:tq])
            o2 = oa[0:DIFF_DV, tq:2 * tq] * (1.0 / oa[DIFF_DV:DIFF_DV + 1, tq:2 * tq])
            d = o1 - lam * o2
            ms = jnp.mean(d * d, axis=0, keepdims=True)
            y = d * lax.rsqrt(ms + EPS) * sg_ref[...] * (1.0 - lam_init)
            mix_ref[g] = y.astype(_BF)
        else:
            oa = jnp.dot(vt_ref[0, vi], p_ref[g % 2, :, 0:tq], preferred_element_type=_F32)
            mix_ref[g] = (oa[0:MLA_V] * (1.0 / oa[MLA_V:MLA_V + 1])).astype(_BF)

    def finish_previous_tile():
        probs(last, m_ref[...])
        mix(maps[last][4], mvt_fin_ref, 0)
        mix_t = mix_ref[...].reshape(MIX_WIDTH, tq)
        yt = jnp.dot(wot_ref[...], mix_t, preferred_element_type=_F32)
        out_ref[0] = x_ref[0] + yt.T

    m_prev = None
    for i in range(N_MAPS):
        m_i = score(i)
        if i == 0:
            finish_previous_tile()
        else:
            probs(i - 1, m_prev)
            g = maps[i - 1][4]
            if maps[i][4] != g:
                mix(g, dvt_ref if g < DIFF_HEADS else mvt_ref, g if g < DIFF_HEADS else g - DIFF_HEADS)
        m_prev = m_i
    m_ref[...] = m_prev


def _ffn_kernel(x_ref, g_ref, wg_ref, wu_ref, wd_ref, fg_ref, out_ref, act_ref, *, final):
    x = x_ref[0]
    h = _rms(x, g_ref[...]).astype(_BF)
    for c in range(D_FF // FF_CHUNK):
        cols = slice(c * FF_CHUNK, (c + 1) * FF_CHUNK)
        gate = jnp.dot(h, wg_ref[:, cols], preferred_element_type=_F32)
        up = jnp.dot(h, wu_ref[:, cols], preferred_element_type=_F32)
        act_ref[:, cols] = (gate * (1.0 / (1.0 + jnp.exp(-gate))) * up).astype(_BF)
    y = x + jnp.dot(act_ref[...], wd_ref[...], preferred_element_type=_F32)
    if final:
        y = _rms(y, fg_ref[...])
    out_ref[0] = y


def _const_spec(shape):
    zeros = (0,) * len(shape)
    return pl.BlockSpec(shape, lambda *_: zeros, pipeline_mode=pl.Buffered(1))


def _params():
    return pltpu.CompilerParams(dimension_semantics=("arbitrary", "arbitrary"),
                                vmem_limit_bytes=VMEM_LIMIT)


def _proj_call(x, w, tabs):
    bsz, seq, _ = x.shape
    tm = min(PROJ_TM, seq)
    grid = (bsz, seq // tm)
    tok = lambda width: pl.BlockSpec((1, tm, width), lambda b, i: (b, i, 0))
    head_tok = lambda nh, width: pl.BlockSpec((1, nh, tm, width), lambda b, i: (b, 0, i, 0))
    head_feat = lambda nh: pl.BlockSpec((1, nh, V_AUG, tm), lambda b, i: (b, 0, 0, i))
    tab = pl.BlockSpec((tm, LANES), lambda b, i: (i, 0))
    consts = [w["attn_g"], w["w_main"], w["w_vt"], w["q_g"], w["w_qb"], w["kv_g"], w["w_kvk"], w["w_kvvt"]]
    return pl.pallas_call(
        _proj_kernel,
        grid=grid,
        in_specs=[tok(D_MODEL)] + [_const_spec(c.shape) for c in consts] + [tab, tab, tab],
        out_specs=[head_tok(2 * DIFF_HEADS, LANES), head_tok(DIFF_HEADS, LANES), head_feat(DIFF_HEADS),
                   head_tok(MLA_HEADS, MLA_QK), head_tok(MLA_HEADS, MLA_QK), head_feat(MLA_HEADS)],
        out_shape=[jax.ShapeDtypeStruct((bsz, 2 * DIFF_HEADS, seq, LANES), _BF),
                   jax.ShapeDtypeStruct((bsz, DIFF_HEADS, seq, LANES), _BF),
                   jax.ShapeDtypeStruct((bsz, DIFF_HEADS, V_AUG, seq), _BF),
                   jax.ShapeDtypeStruct((bsz, MLA_HEADS, seq, MLA_QK), _BF),
                   jax.ShapeDtypeStruct((bsz, MLA_HEADS, seq, MLA_QK), _BF),
                   jax.ShapeDtypeStruct((bsz, MLA_HEADS, V_AUG, seq), _BF)],
        compiler_params=_params(),
        name="proj",
    )(x, *consts, *tabs)


def _attn_call(x, qkv, w, lam_init):
    bsz, seq, _ = x.shape
    tq = min(ATTN_TQ, seq)
    nq = seq // tq
    n_tiles = bsz * nq
    grid = (n_tiles + 1,)
    dq, dk, dvt, mq, mk, mvt = qkv
    att = lambda t: jnp.minimum(t, n_tiles - 1)
    fin = lambda t: jnp.maximum(t - 1, 0)
    tok = pl.BlockSpec((1, tq, D_MODEL), lambda t: (fin(t) // nq, fin(t) % nq, 0))
    q_spec = lambda nh, width: pl.BlockSpec((1, nh, tq, width), lambda t: (att(t) // nq, 0, att(t) % nq, 0))
    full = lambda a: pl.BlockSpec((1,) + a.shape[1:], lambda t: (att(t) // nq, 0, 0, 0))
    fin_v = pl.BlockSpec((1, 1, V_AUG, seq), lambda t: (fin(t) // nq, MLA_HEADS - 1, 0, 0))
    consts = [w["w_ot"], w["subln_g"], w["lam"]]
    return pl.pallas_call(
        functools.partial(_attn_kernel, lam_init=lam_init),
        grid=grid,
        in_specs=[tok, q_spec(2 * DIFF_HEADS, LANES), full(dk), full(dvt),
                  q_spec(MLA_HEADS, MLA_QK), full(mk), full(mvt), fin_v] + [_const_spec(c.shape) for c in consts],
        out_specs=tok,
        out_shape=jax.ShapeDtypeStruct(x.shape, _F32),
        scratch_shapes=[pltpu.VMEM((2, seq, tq), _F32),
                        pltpu.VMEM((2, seq, 2 * tq), _BF),
                        pltpu.VMEM((DIFF_HEADS + MLA_HEADS, DIFF_DV, tq), _BF),
                        pltpu.VMEM((1, tq), _F32)],
        compiler_params=pltpu.CompilerParams(dimension_semantics=("arbitrary",), vmem_limit_bytes=VMEM_LIMIT),
        name="attn",
    )(x, dq, dk, dvt, mq, mk, mvt, mvt, *consts)


def _ffn_call(x, w, final_g, final):
    bsz, seq, _ = x.shape
    tm = min(FFN_TM, seq)
    grid = (bsz, seq // tm)
    tok = pl.BlockSpec((1, tm, D_MODEL), lambda b, i: (b, i, 0))
    consts = [w["ffn_g"], w["w_gate"], w["w_up"], w["w_down"], final_g]
    return pl.pallas_call(
        functools.partial(_ffn_kernel, final=final),
        grid=grid,
        in_specs=[tok] + [_const_spec(c.shape) for c in consts],
        out_specs=tok,
        out_shape=jax.ShapeDtypeStruct(x.shape, _F32),
        scratch_shapes=[pltpu.VMEM((tm, D_FF), _BF)],
        compiler_params=_params(),
        name="ffn",
    )(x, *consts)


def _rope_tables(seq):
    half = DIFF_DH // 2
    inv = 1.0 / (ROPE_THETA ** (jnp.arange(half, dtype=_F32) * 2.0 / DIFF_DH))
    ang = jnp.arange(seq, dtype=jnp.int32).astype(_F32)[:, None] * inv[None, :]
    cos, sin, zero = jnp.cos(ang), jnp.sin(ang), jnp.zeros_like(ang)
    reps = LANES // DIFF_DH
    cos_t = jnp.tile(jnp.concatenate([cos, cos], axis=-1), (1, reps))
    sina_t = jnp.tile(jnp.concatenate([-sin, zero], axis=-1), (1, reps))
    sinb_t = jnp.tile(jnp.concatenate([zero, sin], axis=-1), (1, reps))
    return cos_t, sina_t, sinb_t


def _layer_weights(l, attn_norm, w_in, lam_q1, lam_k1, lam_q2, lam_k2, diff_subln, q_a_norm, w_q_b,
                   kv_a_norm, w_kv_b, w_o, ffn_norm, w_gate, w_up, w_down):
    o1 = DIFF_WIDTH
    o2 = o1 + DIFF_WIDTH
    o3 = o2 + DIFF_WIDTH
    wi = w_in[l]
    pad = jnp.zeros((D_MODEL, LANES - MLA_ROPE), _F32)
    w_main = jnp.concatenate([wi[:, :o2], wi[:, o3:], pad], axis=1).astype(_BF)
    w_vt = wi[:, o2:o3].T.astype(_BF)
    qb = w_q_b[l].reshape(Q_LORA, MLA_HEADS, MLA_NOPE + MLA_ROPE)
    qb = jnp.pad(qb, ((0, 0), (0, 0), (0, MLA_QK - MLA_NOPE - MLA_ROPE)))
    kvb = w_kv_b[l].reshape(KV_LORA, MLA_HEADS, MLA_NOPE + MLA_V)
    return {
        "attn_g": attn_norm[l][None, :],
        "w_main": w_main,
        "w_vt": w_vt,
        "q_g": q_a_norm[l][None, :],
        "w_qb": qb.reshape(Q_LORA, MLA_HEADS * MLA_QK).astype(_BF),
        "kv_g": kv_a_norm[l][None, :],
        "w_kvk": kvb[:, :, :MLA_NOPE].reshape(KV_LORA, MLA_HEADS * MLA_NOPE).astype(_BF),
        "w_kvvt": kvb[:, :, MLA_NOPE:].reshape(KV_LORA, MLA_WIDTH).T.astype(_BF),
        "w_ot": w_o[l].T.astype(_BF),
        "subln_g": diff_subln[l][:, None],
        "lam": jnp.stack([lam_q1[l], lam_k1[l], lam_q2[l], lam_k2[l]]),
        "ffn_g": ffn_norm[l][None, :],
        "w_gate": w_gate[l].astype(_BF),
        "w_up": w_up[l].astype(_BF),
        "w_down": w_down[l].astype(_BF),
    }


def _trunk(x, layers, final_g, tabs):
    for l, w in enumerate(layers):
        lam_init = 0.8 - 0.6 * math.exp(-0.3 * l)
        qkv = _proj_call(x, w, tabs)
        x = _attn_call(x, qkv, w, lam_init)
        x = _ffn_call(x, w, final_g, final=(l == len(layers) - 1))
    return x


def kernel(x_prompt, x_sample, attn_norm, w_in, lam_q1, lam_k1, lam_q2, lam_k2, diff_subln, q_a_norm, w_q_b,
           kv_a_norm, w_kv_b, w_o, ffn_norm, w_gate, w_up, w_down, final_norm):
    layers = [_layer_weights(l, attn_norm, w_in, lam_q1, lam_k1, lam_q2, lam_k2, diff_subln, q_a_norm,
                             w_q_b, kv_a_norm, w_kv_b, w_o, ffn_norm, w_gate, w_up, w_down)
              for l in range(DEPTH)]
    final_g = final_norm[None, :]
    outs = []
    for x in (x_prompt, x_sample):
        outs.append(_trunk(x, layers, final_g, _rope_tables(x.shape[1])))
    return tuple(outs)
```

```python
import functools
import math

import jax
import jax.numpy as jnp
from jax import lax
from jax.experimental import pallas as pl
from jax.experimental.pallas import tpu as pltpu

D_MODEL = 1024
DEPTH = 2
DIFF_HEADS = 4
DIFF_DH = 64
DIFF_DV = 2 * DIFF_DH
DIFF_WIDTH = DIFF_HEADS * DIFF_DV
MLA_HEADS = 4
MLA_NOPE = 128
MLA_ROPE = 64
MLA_V = 128
Q_LORA = 256
KV_LORA = 128
MLA_WIDTH = MLA_HEADS * MLA_V
MIX_WIDTH = DIFF_WIDTH + MLA_WIDTH
D_FF = 2816
ROPE_THETA = 10000.0
EPS = 1e-6

LANES = 128
ROPE_HALF = DIFF_DH // 2
BF16_SUBLANES = 16
MLA_QK = 2 * LANES
V_AUG = DIFF_DV + BF16_SUBLANES
MAIN_COLS = 2 * DIFF_WIDTH + Q_LORA + KV_LORA + LANES
FF_CHUNK = 256
N_MAPS = 2 * DIFF_HEADS + MLA_HEADS
KEY_CHUNK = 1024
LOG2_E = math.log2(math.e)
VMEM_LIMIT = 56 * 1024 * 1024

PROJ_TM = 1024
ATTN_TQ = 256
FFN_TM = 1024

_NT = (((1,), (1,)), ((), ()))
_BF = jnp.bfloat16
_F32 = jnp.float32


def _rms(x, g):
    return x * lax.rsqrt(jnp.mean(x * x, axis=-1, keepdims=True) + EPS) * g


def _proj_kernel(x_ref, g_ref, wmain_ref, wvt_ref, qg_ref, wqb_ref, kvg_ref, wkvk_ref, wkvvt_ref,
                 cos_ref, sin_ref,
                 dq_ref, dk_ref, dvt_ref, mq_ref, mk_ref, mvt_ref):
    tm = x_ref.shape[1]
    h = _rms(x_ref[0], g_ref[...]).astype(_BF)
    z = jnp.dot(h, wmain_ref[...], preferred_element_type=_F32)
    cos = cos_ref[...]
    sin = sin_ref[...]

    def rope(blk):
        return blk * cos + pltpu.roll(blk, LANES // 2, 1) * sin

    first_map = (lax.broadcasted_iota(jnp.int32, (tm, LANES), 1) & ROPE_HALF) == 0
    ones_rows = jnp.ones((BF16_SUBLANES, tm), _BF)
    diff_scale = DIFF_DH ** -0.5 * LOG2_E
    for hh in range(DIFF_HEADS):
        q = rope(z[:, hh * LANES:(hh + 1) * LANES]) * diff_scale
        dq_ref[0, 2 * hh] = jnp.where(first_map, q, 0.0).astype(_BF)
        dq_ref[0, 2 * hh + 1] = jnp.where(first_map, 0.0, q).astype(_BF)
        k = rope(z[:, DIFF_WIDTH + hh * LANES:DIFF_WIDTH + (hh + 1) * LANES])
        dk_ref[0, hh] = k.astype(_BF)
    vt = lax.dot_general(wvt_ref[...], h, _NT, preferred_element_type=_F32)
    for hh in range(DIFF_HEADS):
        dvt_ref[0, hh, 0:DIFF_DV, :] = vt[hh * DIFF_DV:(hh + 1) * DIFF_DV].astype(_BF)
        dvt_ref[0, hh, DIFF_DV:V_AUG, :] = ones_rows

    o3 = 2 * DIFF_WIDTH
    o4 = o3 + Q_LORA
    o5 = o4 + KV_LORA
    cq = _rms(z[:, o3:o4], qg_ref[...]).astype(_BF)
    mla_scale = (MLA_NOPE + MLA_ROPE) ** -0.5 * LOG2_E
    qh = jnp.dot(cq, wqb_ref[...], preferred_element_type=_F32) * mla_scale
    ckv = _rms(z[:, o4:o5], kvg_ref[...]).astype(_BF)
    kn = jnp.dot(ckv, wkvk_ref[...], preferred_element_type=_F32)
    kpe = rope(z[:, o5:o5 + LANES]).astype(_BF)
    mvt = lax.dot_general(wkvvt_ref[...], ckv, _NT, preferred_element_type=_F32)
    for hh in range(MLA_HEADS):
        base = hh * MLA_QK
        mq_ref[0, hh, :, 0:LANES] = qh[:, base:base + LANES].astype(_BF)
        mq_ref[0, hh, :, LANES:MLA_QK] = rope(qh[:, base + LANES:base + MLA_QK]).astype(_BF)
        mk_ref[0, hh, :, 0:LANES] = kn[:, hh * MLA_NOPE:(hh + 1) * MLA_NOPE].astype(_BF)
        mk_ref[0, hh, :, LANES:MLA_QK] = kpe
        mvt_ref[0, hh, 0:MLA_V, :] = mvt[hh * MLA_V:(hh + 1) * MLA_V].astype(_BF)
        mvt_ref[0, hh, MLA_V:V_AUG, :] = ones_rows


def _attn_kernel(x_ref, dq_ref, dk_ref, dvt_ref, mq_ref, mk_ref, mvt_ref, mvt_fin_ref, wot_ref, sg_ref, lam_ref,
                 out_ref, st_ref, p_ref, mix_ref, m_ref, *, lam_init):
    tq = x_ref.shape[1]
    seq = dk_ref.shape[2]
    n_chunks = seq // KEY_CHUNK
    last = N_MAPS - 1

    @pl.when(pl.program_id(0) == 0)
    def _():
        mix_ref[...] = jnp.zeros_like(mix_ref)
        st_ref[last % 2] = jnp.zeros((seq, tq), _F32)
        m_ref[...] = jnp.zeros_like(m_ref)

    lam = (jnp.exp(jnp.sum(lam_ref[0:1, :] * lam_ref[1:2, :], keepdims=True))
           - jnp.exp(jnp.sum(lam_ref[2:3, :] * lam_ref[3:4, :], keepdims=True))
           + lam_init)

    maps = []
    for hh in range(DIFF_HEADS):
        maps.append((dk_ref, hh, dq_ref, 2 * hh, hh, 0))
        maps.append((dk_ref, hh, dq_ref, 2 * hh + 1, hh, 1))
    for hh in range(MLA_HEADS):
        maps.append((mk_ref, hh, mq_ref, hh, DIFF_HEADS + hh, 0))

    def score(i):
        k_ref, ki, q_ref, qi, _, _ = maps[i]
        q = q_ref[0, qi]
        m = None
        for c in range(n_chunks):
            rows = slice(c * KEY_CHUNK, (c + 1) * KEY_CHUNK)
            st = lax.dot_general(k_ref[0, ki, rows, :], q, _NT, preferred_element_type=_F32)
            st_ref[i % 2, rows, :] = st
            mc = jnp.max(st, axis=0, keepdims=True)
            m = mc if m is None else jnp.maximum(m, mc)
        return m

    def probs(i, m):
        _, _, _, _, g, col = maps[i]
        for c in range(n_chunks):
            rows = slice(c * KEY_CHUNK, (c + 1) * KEY_CHUNK)
            p_ref[g % 2, rows, col * tq:(col + 1) * tq] = jnp.exp2(st_ref[i % 2, rows, :] - m).astype(_BF)

    def mix(g, vt_ref, vi):
        if g < DIFF_HEADS:
            oa = jnp.dot(vt_ref[0, vi], p_ref[g % 2], preferred_element_type=_F32)
            o1 = oa[0:DIFF_DV, 0:tq] * (1.0 / oa[DIFF_DV:DIFF_DV + 1, 0:tq])
            o2 = oa[0:DIFF_DV, tq:2 * tq] * (1.0 / oa[DIFF_DV:DIFF_DV + 1, tq:2 * tq])
            d = o1 - lam * o2
            ms = jnp.mean(d * d, axis=0, keepdims=True)
            y = d * lax.rsqrt(ms + EPS) * sg_ref[...] * (1.0 - lam_init)
            mix_ref[g] = y.astype(_BF)
        else:
            oa = jnp.dot(vt_ref[0, vi], p_ref[g % 2, :, 0:tq], preferred_element_type=_F32)
            mix_ref[g] = (oa[0:MLA_V] * (1.0 / oa[MLA_V:MLA_V + 1])).astype(_BF)

    def finish_previous_tile():
        probs(last, m_ref[...])
        mix(maps[last][4], mvt_fin_ref, 0)
        mix_t = mix_ref[...].reshape(MIX_WIDTH, tq)
        yt = jnp.dot(wot_ref[...], mix_t, preferred_element_type=_F32)
        out_ref[0] = x_ref[0] + yt.T

    m_prev = None
    for i in range(N_MAPS):
        m_i = score(i)
        if i == 0:
            finish_previous_tile()
        else:
            probs(i - 1, m_prev)
            g = maps[i - 1][4]
            if maps[i][4] != g:
                mix(g, dvt_ref if g < DIFF_HEADS else mvt_ref, g if g < DIFF_HEADS else g - DIFF_HEADS)
        m_prev = m_i
    m_ref[...] = m_prev


def _ffn_kernel(x_ref, g_ref, wg_ref, wu_ref, wd_ref, fg_ref, out_ref, act_ref, *, final):
    x = x_ref[0]
    h = _rms(x, g_ref[...]).astype(_BF)
    for c in range(D_FF // FF_CHUNK):
        cols = slice(c * FF_CHUNK, (c + 1) * FF_CHUNK)
        gate = jnp.dot(h, wg_ref[:, cols], preferred_element_type=_F32)
        up = jnp.dot(h, wu_ref[:, cols], preferred_element_type=_F32)
        act_ref[:, cols] = (gate * (1.0 / (1.0 + jnp.exp(-gate))) * up).astype(_BF)
    y = x + jnp.dot(act_ref[...], wd_ref[...], preferred_element_type=_F32)
    if final:
        y = _rms(y, fg_ref[...])
    out_ref[0] = y


def _const_spec(shape):
    zeros = (0,) * len(shape)
    return pl.BlockSpec(shape, lambda *_: zeros, pipeline_mode=pl.Buffered(1))


def _params():
    return pltpu.CompilerParams(dimension_semantics=("arbitrary", "arbitrary"),
                                vmem_limit_bytes=VMEM_LIMIT)


def _proj_call(x, w, tabs):
    bsz, seq, _ = x.shape
    tm = min(PROJ_TM, seq)
    grid = (bsz, seq // tm)
    tok = lambda width: pl.BlockSpec((1, tm, width), lambda b, i: (b, i, 0))
    head_tok = lambda nh, width: pl.BlockSpec((1, nh, tm, width), lambda b, i: (b, 0, i, 0))
    head_feat = lambda nh: pl.BlockSpec((1, nh, V_AUG, tm), lambda b, i: (b, 0, 0, i))
    tab = pl.BlockSpec((tm, LANES), lambda b, i: (i, 0))
    consts = [w["attn_g"], w["w_main"], w["w_vt"], w["q_g"], w["w_qb"], w["kv_g"], w["w_kvk"], w["w_kvvt"]]
    return pl.pallas_call(
        _proj_kernel,
        grid=grid,
        in_specs=[tok(D_MODEL)] + [_const_spec(c.shape) for c in consts] + [tab, tab],
        out_specs=[head_tok(2 * DIFF_HEADS, LANES), head_tok(DIFF_HEADS, LANES), head_feat(DIFF_HEADS),
                   head_tok(MLA_HEADS, MLA_QK), head_tok(MLA_HEADS, MLA_QK), head_feat(MLA_HEADS)],
        out_shape=[jax.ShapeDtypeStruct((bsz, 2 * DIFF_HEADS, seq, LANES), _BF),
                   jax.ShapeDtypeStruct((bsz, DIFF_HEADS, seq, LANES), _BF),
                   jax.ShapeDtypeStruct((bsz, DIFF_HEADS, V_AUG, seq), _BF),
                   jax.ShapeDtypeStruct((bsz, MLA_HEADS, seq, MLA_QK), _BF),
                   jax.ShapeDtypeStruct((bsz, MLA_HEADS, seq, MLA_QK), _BF),
                   jax.ShapeDtypeStruct((bsz, MLA_HEADS, V_AUG, seq), _BF)],
        compiler_params=_params(),
        name="proj",
    )(x, *consts, *tabs)


def _attn_call(x, qkv, w, lam_init):
    bsz, seq, _ = x.shape
    tq = min(ATTN_TQ, seq)
    nq = seq // tq
    n_tiles = bsz * nq
    grid = (n_tiles + 1,)
    dq, dk, dvt, mq, mk, mvt = qkv
    att = lambda t: jnp.minimum(t, n_tiles - 1)
    fin = lambda t: jnp.maximum(t - 1, 0)
    tok = pl.BlockSpec((1, tq, D_MODEL), lambda t: (fin(t) // nq, fin(t) % nq, 0))
    q_spec = lambda nh, width: pl.BlockSpec((1, nh, tq, width), lambda t: (att(t) // nq, 0, att(t) % nq, 0))
    full = lambda a: pl.BlockSpec((1,) + a.shape[1:], lambda t: (att(t) // nq, 0, 0, 0))
    fin_v = pl.BlockSpec((1, 1, V_AUG, seq), lambda t: (fin(t) // nq, MLA_HEADS - 1, 0, 0))
    consts = [w["w_ot"], w["subln_g"], w["lam"]]
    return pl.pallas_call(
        functools.partial(_attn_kernel, lam_init=lam_init),
        grid=grid,
        in_specs=[tok, q_spec(2 * DIFF_HEADS, LANES), full(dk), full(dvt),
                  q_spec(MLA_HEADS, MLA_QK), full(mk), full(mvt), fin_v] + [_const_spec(c.shape) for c in consts],
        out_specs=tok,
        out_shape=jax.ShapeDtypeStruct(x.shape, _F32),
        scratch_shapes=[pltpu.VMEM((2, seq, tq), _F32),
                        pltpu.VMEM((2, seq, 2 * tq), _BF),
                        pltpu.VMEM((DIFF_HEADS + MLA_HEADS, DIFF_DV, tq), _BF),
                        pltpu.VMEM((1, tq), _F32)],
        compiler_params=pltpu.CompilerParams(dimension_semantics=("arbitrary",), vmem_limit_bytes=VMEM_LIMIT),
        name="attn",
    )(x, dq, dk, dvt, mq, mk, mvt, mvt, *consts)


def _ffn_call(x, w, final_g, final):
    bsz, seq, _ = x.shape
    tm = min(FFN_TM, seq)
    grid = (bsz, seq // tm)
    tok = pl.BlockSpec((1, tm, D_MODEL), lambda b, i: (b, i, 0))
    consts = [w["ffn_g"], w["w_gate"], w["w_up"], w["w_down"], final_g]
    return pl.pallas_call(
        functools.partial(_ffn_kernel, final=final),
        grid=grid,
        in_specs=[tok] + [_const_spec(c.shape) for c in consts],
        out_specs=tok,
        out_shape=jax.ShapeDtypeStruct(x.shape, _F32),
        scratch_shapes=[pltpu.VMEM((tm, D_FF), _BF)],
        compiler_params=_params(),
        name="ffn",
    )(x, *consts)


def _rope_tables(seq):
    half = DIFF_DH // 2
    inv = 1.0 / (ROPE_THETA ** (jnp.arange(half, dtype=_F32) * 2.0 / DIFF_DH))
    ang = jnp.arange(seq, dtype=jnp.int32).astype(_F32)[:, None] * inv[None, :]
    cos, sin = jnp.cos(ang), jnp.sin(ang)
    cos_t = jnp.tile(cos, (1, LANES // half))
    sin_t = jnp.concatenate([-sin, -sin, sin, sin], axis=-1)
    return cos_t, sin_t


def _pair_rotary_cols(w, n_blocks):
    rows = w.shape[0]
    w = w.reshape(rows, n_blocks, 2, 2, ROPE_HALF)
    return w.transpose(0, 1, 3, 2, 4).reshape(rows, n_blocks * LANES)


def _spread_rotary_cols(w):
    w = w.reshape(w.shape[:-1] + (2, ROPE_HALF))
    w = jnp.pad(w, [(0, 0)] * (w.ndim - 1) + [(0, ROPE_HALF)])
    return w.reshape(w.shape[:-2] + (LANES,))


def _layer_weights(l, attn_norm, w_in, lam_q1, lam_k1, lam_q2, lam_k2, diff_subln, q_a_norm, w_q_b,
                   kv_a_norm, w_kv_b, w_o, ffn_norm, w_gate, w_up, w_down):
    o1 = DIFF_WIDTH
    o2 = o1 + DIFF_WIDTH
    o3 = o2 + DIFF_WIDTH
    wi = w_in[l]
    o5 = o3 + Q_LORA + KV_LORA
    w_main = jnp.concatenate([_pair_rotary_cols(wi[:, :o2], 2 * DIFF_HEADS), wi[:, o3:o5],
                              _spread_rotary_cols(wi[:, o5:])], axis=1).astype(_BF)
    w_vt = wi[:, o2:o3].T.astype(_BF)
    qb = w_q_b[l].reshape(Q_LORA, MLA_HEADS, MLA_NOPE + MLA_ROPE)
    qb = jnp.concatenate([qb[:, :, :MLA_NOPE], _spread_rotary_cols(qb[:, :, MLA_NOPE:])], axis=-1)
    kvb = w_kv_b[l].reshape(KV_LORA, MLA_HEADS, MLA_NOPE + MLA_V)
    return {
        "attn_g": attn_norm[l][None, :],
        "w_main": w_main,
        "w_vt": w_vt,
        "q_g": q_a_norm[l][None, :],
        "w_qb": qb.reshape(Q_LORA, MLA_HEADS * MLA_QK).astype(_BF),
        "kv_g": kv_a_norm[l][None, :],
        "w_kvk": kvb[:, :, :MLA_NOPE].reshape(KV_LORA, MLA_HEADS * MLA_NOPE).astype(_BF),
        "w_kvvt": kvb[:, :, MLA_NOPE:].reshape(KV_LORA, MLA_WIDTH).T.astype(_BF),
        "w_ot": w_o[l].T.astype(_BF),
        "subln_g": diff_subln[l][:, None],
        "lam": jnp.stack([lam_q1[l], lam_k1[l], lam_q2[l], lam_k2[l]]),
        "ffn_g": ffn_norm[l][None, :],
        "w_gate": w_gate[l].astype(_BF),
        "w_up": w_up[l].astype(_BF),
        "w_down": w_down[l].astype(_BF),
    }


def _trunk(x, layers, final_g, tabs):
    for l, w in enumerate(layers):
        lam_init = 0.8 - 0.6 * math.exp(-0.3 * l)
        qkv = _proj_call(x, w, tabs)
        x = _attn_call(x, qkv, w, lam_init)
        x = _ffn_call(x, w, final_g, final=(l == len(layers) - 1))
    return x


def kernel(x_prompt, x_sample, attn_norm, w_in, lam_q1, lam_k1, lam_q2, lam_k2, diff_subln, q_a_norm, w_q_b,
           kv_a_norm, w_kv_b, w_o, ffn_norm, w_gate, w_up, w_down, final_norm):
    layers = [_layer_weights(l, attn_norm, w_in, lam_q1, lam_k1, lam_q2, lam_k2, diff_subln, q_a_norm,
                             w_q_b, kv_a_norm, w_kv_b, w_o, ffn_norm, w_gate, w_up, w_down)
              for l in range(DEPTH)]
    final_g = final_norm[None, :]
    outs = []
    for x in (x_prompt, x_sample):
        outs.append(_trunk(x, layers, final_g, _rope_tables(x.shape[1])))
    return tuple(outs)
```
